```python
import jax, jax.numpy as jnp
from jax import lax
import numpy as np

D_MODEL = 1024
BATCH = 1
SEQ = 16384
DEPTH = 1
DEC_BATCH = 32
DEC_SEQ = 4
PAST_LEN = 16384
PAGE_SIZE = 128

HEAD_DIM = 64
NORM_EPS = 1e-6
NEG_INF = -1e30
RW_HEADS = 8
RW_WIDTH = RW_HEADS * HEAD_DIM
LORA_W = 64
LORA_A = 64
LORA_G = 128
RW_COLS = 3 * RW_WIDTH + LORA_W + LORA_A + LORA_G
GN_EPS = 64e-5
ATT_GROUPS = ((128, 1), (512, 4), (2048, 16))
HEADS_PER_GROUP = 4
ATT_HEADS = HEADS_PER_GROUP * len(ATT_GROUPS)
ATT_WIDTH = ATT_HEADS * HEAD_DIM
ATT_COLS = 3 * ATT_WIDTH
ATT_OUT = HEADS_PER_GROUP * HEAD_DIM
ROT_DIM = HEAD_DIM // 4
ROPE_THETA = 500000.0
GATE_COLS = 2 * D_MODEL
IN_COLS = RW_COLS + ATT_COLS + GATE_COLS
PEER_HEADS = 8
PEER_NKEYS = 128
PEER_EXPERTS = PEER_NKEYS * PEER_NKEYS
PEER_DKEY = 256
PEER_TOPK = 16
PEER_BLOCK = 128

kernel_name = 'hybrid_rwkv7_dilated_peer_step'


def rmsnorm(x, w):
    x32 = x.astype(jnp.float32)
    y = x32 * lax.rsqrt(jnp.mean(x32 * x32, axis=-1, keepdims=True) + NORM_EPS)
    return (y * w.astype(jnp.float32)).astype(x.dtype)


def partial_rope(x, pos):
    half = ROT_DIM // 2
    inv_freq = ROPE_THETA ** (-jnp.arange(half, dtype=jnp.float32) * (2.0 / ROT_DIM))
    ang = pos[:, None] * inv_freq[None, :]
    cos = jnp.cos(ang)[None, :, None, :]
    sin = jnp.sin(ang)[None, :, None, :]
    x32 = x.astype(jnp.float32)
    x1, x2, rest = x32[..., :half], x32[..., half:ROT_DIM], x32[..., ROT_DIM:]
    return jnp.concatenate([x1 * cos - x2 * sin, x1 * sin + x2 * cos, rest], axis=-1).astype(x.dtype)


def rwkv_step(S, inp):
    r, w, k, v, kk, b = inp
    sa = jnp.einsum('bhvk,bhk->bhv', S, kk)
    S = S * w[:, :, None, :] - sa[..., None] * b[:, :, None, :] + v[..., None] * k[:, :, None, :]
    return S, jnp.einsum('bhvk,bhk->bhv', S, r)


def rwkv7_branch(cols, prev_row, wkv0, mu_shift, w0, w2, a0, a2, g2, k_k, k_a, r_k, ln_x_w, ln_x_b):
    B, T, _ = cols.shape
    f32 = jnp.float32
    prev = jnp.concatenate([prev_row[:, None, :].astype(cols.dtype), cols[:, :-1]], axis=1)
    xs = cols + (prev - cols) * mu_shift
    i1, i2, i3 = RW_WIDTH, 2 * RW_WIDTH, 3 * RW_WIDTH
    i4, i5 = i3 + LORA_W, i3 + LORA_W + LORA_A
    r, k, v, wl, al, gl = jnp.split(xs, [i1, i2, i3, i4, i5], axis=-1)
    w_log = -jax.nn.softplus(-(w0 + jnp.tanh(wl) @ w2).astype(f32)) - 0.5
    decay = jnp.exp(-jnp.exp(w_log))
    a = jax.nn.sigmoid((a0 + al @ a2).astype(f32))
    g = jax.nn.sigmoid(gl) @ g2

    def heads(t):
        return t.astype(f32).reshape(B, T, RW_HEADS, HEAD_DIM)

    r_h, v_h, w_h, a_h = heads(r), heads(v), heads(decay), heads(a)
    kk = heads(k * k_k)
    kk = kk * lax.rsqrt(jnp.maximum(jnp.sum(kk * kk, axis=-1, keepdims=True), 1e-24))
    k_h = heads(k) * (1.0 + (a_h - 1.0) * k_a.astype(f32).reshape(RW_HEADS, HEAD_DIM))
    b_h = kk * a_h
    seq = tuple(jnp.moveaxis(t, 1, 0) for t in (r_h, w_h, k_h, v_h, kk, b_h))
    S_fin, o = lax.scan(rwkv_step, wkv0.astype(f32), seq)
    o = jnp.moveaxis(o, 0, 1)
    mu = jnp.mean(o, axis=-1, keepdims=True)
    var = jnp.mean(jnp.square(o - mu), axis=-1, keepdims=True)
    o = (o - mu) * lax.rsqrt(var + GN_EPS)
    o = o * ln_x_w.astype(f32).reshape(RW_HEADS, HEAD_DIM) + ln_x_b.astype(f32).reshape(RW_HEADS, HEAD_DIM)
    o = o + jnp.sum(r_h * k_h * r_k.astype(f32), axis=-1, keepdims=True) * v_h
    out = o.reshape(B, T, RW_WIDTH).astype(cols.dtype) * g
    return out, S_fin.astype(wkv0.dtype), cols[:, -1]


def dilated_band_attention(q, k, v, dil, n_steps):
    B, S, H, Dh = q.shape
    W = n_steps
    L = S // dil
    nb = -(-L // W)
    Lp = nb * W

    def blocks(t):
        t = t.astype(jnp.float32).reshape(B, L, dil, H, Dh)
        t = jnp.pad(t, ((0, 0), (0, Lp - L), (0, 0), (0, 0), (0, 0)))
        return t.reshape(B, nb, W, dil, H, Dh)

    def with_prev(t):
        prev = jnp.pad(t, ((0, 0), (1, 0), (0, 0), (0, 0), (0, 0), (0, 0)))[:, :nb]
        return jnp.concatenate([prev, t], axis=2)

    qb = blocks(q)
    kb = with_prev(blocks(k))
    vb = with_prev(blocks(v))
    s = jnp.einsum('bnqrhd,bnkrhd->bnrhqk', qb, kb) * (Dh ** -0.5)
    qi = jnp.arange(W)[:, None]
    kj = jnp.arange(2 * W)[None, :]
    band = (kj >= qi) & (kj <= qi + W)
    has_prev = (jnp.arange(nb)[:, None, None] > 0) | (kj[None] >= W)
    mask = (band[None] & has_prev)[None, :, None, None]
    s = jnp.where(mask, s, NEG_INF)
    m = jnp.max(s, axis=-1)
    p = jnp.exp(s - m[..., None])
    den = jnp.sum(p, axis=-1)
    m_t = jnp.transpose(m, (0, 1, 4, 2, 3))
    den_t = jnp.transpose(den, (0, 1, 4, 2, 3))
    o = jnp.einsum('bnrhqk,bnkrhd->bnqrhd', p, vb) / den_t[..., None]

    def unblock(t):
        return t.reshape((B, Lp * dil) + t.shape[4:])[:, :S]

    return unblock(o), unblock(m_t), unblock(den_t)


def dilated_gather_attention(q, kv_cat, dil, n_steps, buf_len):
    B, T, H, Dh = q.shape
    idx = buf_len + jnp.arange(T)[:, None] - dil * jnp.arange(n_steps + 1)[None, :]
    valid = idx >= 0
    kvg = kv_cat[:, jnp.maximum(idx, 0)].astype(jnp.float32)
    s = jnp.einsum('bthd,btchd->bthc', q.astype(jnp.float32), kvg[:, :, :, 0]) * (Dh ** -0.5)
    s = jnp.where(valid[None, :, None, :], s, NEG_INF)
    m = jnp.max(s, axis=-1)
    p = jnp.exp(s - m[..., None])
    den = jnp.sum(p, axis=-1)
    o = jnp.einsum('bthc,btchd->bthd', p, kvg[:, :, :, 1]) / den[..., None]
    return o, m, den


def combine_groups(outs, ms, dens):
    o = jnp.stack(outs)
    m = jnp.stack(ms)
    den = jnp.stack(dens)
    wgt = den * jnp.exp(m - jnp.max(m, axis=0, keepdims=True))
    y = jnp.sum(wgt[..., None] * o, axis=0) / jnp.sum(wgt, axis=0)[..., None]
    return y.reshape(y.shape[:2] + (ATT_OUT,))


def split_qkv(att_cols, pos):
    B, T, _ = att_cols.shape
    q, k, v = jnp.split(att_cols, 3, axis=-1)
    shp = (B, T, ATT_HEADS, HEAD_DIM)
    return partial_rope(q.reshape(shp), pos), partial_rope(k.reshape(shp), pos), v.reshape(shp)


def peer_ffn(x, w_pq, sub_keys_1, sub_keys_2, expert_u, expert_v):
    B, T, D = x.shape
    N = B * T
    nb = -(-N // PEER_BLOCK)
    xt = jnp.pad(x.reshape(N, D), ((0, nb * PEER_BLOCK - N), (0, 0))).reshape(nb, PEER_BLOCK, D)
    half = PEER_DKEY // 2

    def block(xb):
        q = (xb @ w_pq).reshape(PEER_BLOCK, PEER_HEADS, PEER_DKEY).astype(jnp.float32)
        s1 = jnp.einsum('nhd,kd->nhk', q[..., :half], sub_keys_1.astype(jnp.float32))
        s2 = jnp.einsum('nhd,kd->nhk', q[..., half:], sub_keys_2.astype(jnp.float32))
        v1, i1 = lax.top_k(s1, PEER_TOPK)
        v2, i2 = lax.top_k(s2, PEER_TOPK)
        cand = (v1[..., :, None] + v2[..., None, :]).reshape(PEER_BLOCK, PEER_HEADS, PEER_TOPK * PEER_TOPK)
        vs, ci = lax.top_k(cand, PEER_TOPK)
        eid = (jnp.take_along_axis(i1, ci // PEER_TOPK, axis=-1) * PEER_NKEYS
               + jnp.take_along_axis(i2, ci % PEER_TOPK, axis=-1))
        gate = jax.nn.softmax(vs, axis=-1)
        act = jax.nn.gelu(jnp.einsum('nd,nhed->nhe', xb, expert_u[eid]), approximate=False)
        return jnp.einsum('nhe,nhed->nd', (gate * act).astype(xb.dtype), expert_v[eid])

    out = lax.map(block, xt)
    return out.reshape(nb * PEER_BLOCK, D)[:N].reshape(B, T, D)


def finish_layer(x, oA, oB, gate_cols, w_oA, w_oB, w_out, norm2_w, w_pq, sub_keys_1, sub_keys_2, expert_u, expert_v):
    gA, gB = jnp.split(gate_cols, 2, axis=-1)
    merged = jax.nn.sigmoid(gA) * (oA @ w_oA) + jax.nn.sigmoid(gB) * (oB.astype(x.dtype) @ w_oB)
    x = x + merged @ w_out
    return x + peer_ffn(rmsnorm(x, norm2_w), w_pq, sub_keys_1, sub_keys_2, expert_u, expert_v)


def prompt_layer(x, norm1_w, w_in, rw_params, out_params):
    B, S, _ = x.shape
    proj = rmsnorm(x, norm1_w) @ w_in
    rw_cols, att_cols, gate_cols = jnp.split(proj, [RW_COLS, RW_COLS + ATT_COLS], axis=-1)
    oA, wkv, shift = rwkv7_branch(rw_cols, jnp.zeros((B, RW_COLS), x.dtype),
                                  jnp.zeros((B, RW_HEADS, HEAD_DIM, HEAD_DIM), x.dtype), *rw_params)
    q, k, v = split_qkv(att_cols, jnp.arange(S, dtype=jnp.float32))
    outs, ms, dens, kv_rows = [], [], [], []
    for g, (win, dil) in enumerate(ATT_GROUPS):
        hs = slice(g * HEADS_PER_GROUP, (g + 1) * HEADS_PER_GROUP)
        o, m, den = dilated_band_attention(q[:, :, hs], k[:, :, hs], v[:, :, hs], dil, win // dil)
        outs.append(o)
        ms.append(m)
        dens.append(den)
        keep = min(win, S)
        kv_rows.append(jnp.stack([k[:, S - keep:, hs], v[:, S - keep:, hs]], axis=2))
    oB = combine_groups(outs, ms, dens)
    x = finish_layer(x, oA, oB, gate_cols, *out_params)
    return x, kv_rows, wkv, shift


def sample_layer(x, caches, wkv0, shift0, norm1_w, w_in, rw_params, out_params):
    T = x.shape[1]
    proj = rmsnorm(x, norm1_w) @ w_in
    rw_cols, att_cols, gate_cols = jnp.split(proj, [RW_COLS, RW_COLS + ATT_COLS], axis=-1)
    oA, wkv, shift = rwkv7_branch(rw_cols, shift0, wkv0, *rw_params)
    q, k, v = split_qkv(att_cols, jnp.arange(T, dtype=jnp.float32) + PAST_LEN)
    outs, ms, dens, kv_rows = [], [], [], []
    for g, ((win, dil), cache) in enumerate(zip(ATT_GROUPS, caches)):
        hs = slice(g * HEADS_PER_GROUP, (g + 1) * HEADS_PER_GROUP)
        new_kv = jnp.stack([k[:, :, hs], v[:, :, hs]], axis=2)
        kv_cat = jnp.concatenate([cache.astype(new_kv.dtype), new_kv], axis=1)
        o, m, den = dilated_gather_attention(q[:, :, hs], kv_cat, dil, win // dil, cache.shape[1])
        outs.append(o)
        ms.append(m)
        dens.append(den)
        kv_rows.append(new_kv)
    oB = combine_groups(outs, ms, dens)
    x = finish_layer(x, oA, oB, gate_cols, *out_params)
    return x, kv_rows, wkv, shift


def setup_inputs(seed: int = 0) -> dict:
    key = jax.random.key(seed)
    ks = jax.random.split(key, 32)
    f32 = jnp.float32

    def nrm(k, shape, scale):
        return jax.random.normal(k, shape, f32) * scale

    def kv_buf(k, win):
        return nrm(k, (DEC_BATCH, min(win, PAST_LEN), 2, HEADS_PER_GROUP, HEAD_DIM), 1.0)

    return {
        'x_prompt': nrm(ks[0], (BATCH, SEQ, D_MODEL), 1.0),
        'x_sample': nrm(ks[1], (DEC_BATCH, DEC_SEQ, D_MODEL), 1.0),
        'cache_kv_g1': kv_buf(ks[2], ATT_GROUPS[0][0]),
        'cache_kv_g2': kv_buf(ks[3], ATT_GROUPS[1][0]),
        'cache_kv_g3': kv_buf(ks[4], ATT_GROUPS[2][0]),
        'state_wkv': nrm(ks[5], (DEC_BATCH, RW_HEADS, HEAD_DIM, HEAD_DIM), 0.3),
        'state_shift': nrm(ks[6], (DEC_BATCH, RW_COLS), 1.0),
        'norm1_w': 1.0 + nrm(ks[7], (D_MODEL,), 0.01),
        'w_in': nrm(ks[8], (D_MODEL, IN_COLS), D_MODEL ** -0.5),
        'mu_shift': jax.random.uniform(ks[9], (RW_COLS,), f32),
        'w0': nrm(ks[10], (RW_WIDTH,), 1.0),
        'w2': nrm(ks[11], (LORA_W, RW_WIDTH), LORA_W ** -0.5),
        'a0': nrm(ks[12], (RW_WIDTH,), 0.5),
        'a2': nrm(ks[13], (LORA_A, RW_WIDTH), LORA_A ** -0.5),
        'g2': nrm(ks[14], (LORA_G, RW_WIDTH), LORA_G ** -0.5),
        'k_k': 0.85 + nrm(ks[15], (RW_WIDTH,), 0.02),
        'k_a': 1.0 + nrm(ks[16], (RW_WIDTH,), 0.02),
        'r_k': nrm(ks[17], (RW_HEADS, HEAD_DIM), 0.1),
        'ln_x_w': 1.0 + nrm(ks[18], (RW_WIDTH,), 0.01),
        'ln_x_b': nrm(ks[19], (RW_WIDTH,), 0.01),
        'w_oA': nrm(ks[20], (RW_WIDTH, D_MODEL), RW_WIDTH ** -0.5),
        'w_oB': nrm(ks[21], (ATT_OUT, D_MODEL), ATT_OUT ** -0.5),
        'w_out': nrm(ks[22], (D_MODEL, D_MODEL), D_MODEL ** -0.5),
        'norm2_w': 1.0 + nrm(ks[23], (D_MODEL,), 0.01),
        'w_pq': nrm(ks[24], (D_MODEL, PEER_HEADS * PEER_DKEY), D_MODEL ** -0.5),
        'sub_keys_1': nrm(ks[25], (PEER_NKEYS, PEER_DKEY // 2), (PEER_DKEY // 2) ** -0.5),
        'sub_keys_2': nrm(ks[26], (PEER_NKEYS, PEER_DKEY // 2), (PEER_DKEY // 2) ** -0.5),
        'expert_u': nrm(ks[27], (PEER_EXPERTS, D_MODEL), D_MODEL ** -0.5),
        'expert_v': nrm(ks[28], (PEER_EXPERTS, D_MODEL), 0.3),
        'normf_w': 1.0 + nrm(ks[29], (D_MODEL,), 0.01),
    }


def reference(x_prompt, x_sample, cache_kv_g1, cache_kv_g2, cache_kv_g3, state_wkv, state_shift,
              norm1_w, w_in, mu_shift, w0, w2, a0, a2, g2, k_k, k_a, r_k, ln_x_w, ln_x_b,
              w_oA, w_oB, w_out, norm2_w, w_pq, sub_keys_1, sub_keys_2, expert_u, expert_v, normf_w):
    rw_params = (mu_shift, w0, w2, a0, a2, g2, k_k, k_a, r_k, ln_x_w, ln_x_b)
    out_params = (w_oA, w_oB, w_out, norm2_w, w_pq, sub_keys_1, sub_keys_2, expert_u, expert_v)
    xp, xs = x_prompt, x_sample
    for _layer in range(DEPTH):
        xp, p_kv, p_wkv, p_shift = prompt_layer(xp, norm1_w, w_in, rw_params, out_params)
        xs, s_kv, s_wkv, s_shift = sample_layer(xs, (cache_kv_g1, cache_kv_g2, cache_kv_g3), state_wkv,
                                                state_shift, norm1_w, w_in, rw_params, out_params)
    y_prompt = rmsnorm(xp, normf_w)
    y_sample = rmsnorm(xs, normf_w)
    p_kv1, p_kv2, p_kv3 = p_kv
    s_kv1, s_kv2, s_kv3 = s_kv
    return (y_prompt, y_sample, p_kv1, p_kv2, p_kv3, p_wkv, p_shift, s_kv1, s_kv2, s_kv3, s_wkv, s_shift)
```

```python
import functools
import math

import jax
import jax.numpy as jnp
from jax import lax
from jax.experimental import pallas as pl
from jax.experimental.pallas import tpu as pltpu

F32 = jnp.float32
BF16 = jnp.bfloat16

D_MODEL = 1024
HEAD_DIM = 64
NORM_EPS = 1e-6
NEG_INF = -1e30
RW_HEADS = 8
RW_WIDTH = RW_HEADS * HEAD_DIM
LORA_W = 64
LORA_A = 64
LORA_G = 128
RW_COLS = 3 * RW_WIDTH + LORA_W + LORA_A + LORA_G
GN_EPS = 64e-5
ATT_GROUPS = ((128, 1), (512, 4), (2048, 16))
HEADS_PER_GROUP = 4
ATT_HEADS = HEADS_PER_GROUP * len(ATT_GROUPS)
ATT_WIDTH = ATT_HEADS * HEAD_DIM
ATT_COLS = 3 * ATT_WIDTH
ATT_OUT = HEADS_PER_GROUP * HEAD_DIM
ROT_DIM = HEAD_DIM // 4
ROPE_THETA = 500000.0
GATE_COLS = 2 * D_MODEL
IN_COLS = RW_COLS + ATT_COLS + GATE_COLS
PEER_HEADS = 8
PEER_NKEYS = 128
PEER_DKEY = 256
PEER_TOPK = 16

DEC_SEQ = 4
PAST_LEN = 16384

LANES = 128
_T_PAD = 8
_RW_CHUNK = 64
Q_COL0 = RW_COLS
K_COL0 = RW_COLS + ATT_WIDTH
V_COL0 = RW_COLS + 2 * ATT_WIDTH
GATE_COL0 = RW_COLS + ATT_COLS
VMEM_LIMIT = 56 * 1024 * 1024

_NT = (((1,), (1,)), ((), ()))
_TN = (((0,), (0,)), ((), ()))
_NN = (((1,), (0,)), ((), ()))


def _dot1(a, b, dims=_NN):
    return lax.dot_general(a.astype(BF16), b.astype(BF16), dims, preferred_element_type=F32)


def _split(x):
    hi = x.astype(BF16)
    lo = (x - hi.astype(F32)).astype(BF16)
    return hi, lo


def _dot3(a, b, dims=_NN):
    ah, al = _split(a)
    bh, bl = _split(b)
    dg = functools.partial(lax.dot_general, dimension_numbers=dims, preferred_element_type=F32)
    return dg(ah, bh) + (dg(ah, bl) + dg(al, bh))


def _sigmoid(x):
    return 1.0 / (1.0 + jnp.exp(-x))


_A_TN = 256


def _inproj_body(x_ref, nw_ref, w_ref, c_ref, sa_ref, sb_ref, o_ref, h_scr):
    j = pl.program_id(1)

    @pl.when(j == 0)
    def _():
        x = x_ref[...]
        ms = jnp.mean(x * x, axis=-1, keepdims=True)
        h_scr[...] = (x * lax.rsqrt(ms + NORM_EPS) * nw_ref[...]).astype(BF16)

    acc = jnp.dot(h_scr[...], w_ref[...], preferred_element_type=F32)
    is_rope = jnp.logical_and(j >= Q_COL0 // _A_TN, j < V_COL0 // _A_TN)

    @pl.when(is_rope)
    def _():
        c, sa, sb = c_ref[...], sa_ref[...], sb_ref[...]
        for t in range(_A_TN // LANES):
            xh = acc[:, t * LANES:(t + 1) * LANES]
            o_ref[:, t * LANES:(t + 1) * LANES] = (
                xh * c + pltpu.roll(xh, LANES - ROT_DIM // 2, 1) * sa + pltpu.roll(xh, ROT_DIM // 2, 1) * sb)

    @pl.when(jnp.logical_not(is_rope))
    def _():
        o_ref[...] = acc


def _rope_tables(pos):
    half = ROT_DIM // 2
    inv_freq = ROPE_THETA ** (-jnp.arange(half, dtype=F32) * (2.0 / ROT_DIM))
    ang = pos[:, None] * inv_freq[None, :]
    cos, sin = jnp.cos(ang), jnp.sin(ang)
    t = pos.shape[0]
    one = jnp.ones((t, HEAD_DIM - ROT_DIM), F32)
    zero = jnp.zeros((t, HEAD_DIM - ROT_DIM), F32)
    z8 = jnp.zeros((t, half), F32)
    c = jnp.concatenate([cos, cos, one], axis=1)
    sa = jnp.concatenate([-sin, z8, zero], axis=1)
    sb = jnp.concatenate([z8, sin, zero], axis=1)
    rep = LANES // HEAD_DIM
    return tuple(jnp.tile(a, (1, rep)) for a in (c, sa, sb))


def _in_proj(x2d, pos, norm1_w, w_in_bf16):
    n = x2d.shape[0]
    tm = min(n, 1024)
    assert n % tm == 0 and Q_COL0 % _A_TN == 0 and V_COL0 % _A_TN == 0
    c, sa, sb = _rope_tables(pos)
    tab_spec = pl.BlockSpec((tm, LANES), lambda i, j: (i, 0))
    return pl.pallas_call(
        _inproj_body,
        grid=(n // tm, IN_COLS // _A_TN),
        in_specs=[
            pl.BlockSpec((tm, D_MODEL), lambda i, j: (i, 0)),
            pl.BlockSpec((1, D_MODEL), lambda i, j: (0, 0)),
            pl.BlockSpec((D_MODEL, _A_TN), lambda i, j: (0, j)),
            tab_spec, tab_spec, tab_spec,
        ],
        out_specs=pl.BlockSpec((tm, _A_TN), lambda i, j: (i, j)),
        out_shape=jax.ShapeDtypeStruct((n, IN_COLS), F32),
        scratch_shapes=[pltpu.VMEM((tm, D_MODEL), BF16)],
        compiler_params=pltpu.CompilerParams(
            dimension_semantics=("arbitrary", "arbitrary"), vmem_limit_bytes=VMEM_LIMIT),
    )(x2d, norm1_w.reshape(1, D_MODEL), w_in_bf16, c, sa, sb)


def _cumsum_rows(x, c):
    row = lax.broadcasted_iota(jnp.int32, x.shape, 0)
    s = 1
    while s < c:
        x = x + jnp.where(row >= s, pltpu.roll(x, s, 0), 0.0)
        s *= 2
    return x


def _unit_lower_inverse(n_mat, c):
    row = lax.broadcasted_iota(jnp.int32, (c, c), 0)
    col = lax.broadcasted_iota(jnp.int32, (c, c), 1)
    x = jnp.where(row == col, 1.0, 0.0) - n_mat
    p = _dot3(n_mat, n_mat)
    n = 2
    while n < c:
        x = x + _dot3(x, p)
        n *= 2
        if n < c:
            p = _dot3(p, p)
    return x


def _rwkv_head(r, k, v, kk, b, lw, s0, c):
    cum = _cumsum_rows(lw, c)
    cl = cum[c - 1:c, :]
    e_pos = jnp.exp(cum)
    e_neg = jnp.exp(-cum)
    e_tail = jnp.exp(cl - cum)
    kkt = kk * jnp.exp(cum - lw)
    rt = r * e_pos
    kh = k * e_neg
    bh = b * e_neg
    kb = k * e_tail
    bb = b * e_tail
    p_c = jnp.exp(cl)

    sc = _dot3(jnp.concatenate([kkt, rt], axis=0), jnp.concatenate([kh, bh], axis=0), _NT)
    row = lax.broadcasted_iota(jnp.int32, (c, c), 0)
    col = lax.broadcasted_iota(jnp.int32, (c, c), 1)
    strict = row > col
    incl = row >= col
    m_kv = jnp.where(strict, sc[:c, :c], 0.0)
    m_kb = jnp.where(strict, sc[:c, c:], 0.0)
    a_rk = jnp.where(incl, sc[c:, :c], 0.0)
    a_rb = jnp.where(incl, sc[c:, c:], 0.0)

    t_inv = _unit_lower_inverse(m_kb, c)
    mv = _dot3(m_kv, v)
    wy = _dot3(t_inv, jnp.concatenate([kkt, mv], axis=1))
    arb_wy = _dot3(a_rb, wy)
    rq = rt - arb_wy[:, :HEAD_DIM]
    o_loc = _dot3(a_rk, v) - arb_wy[:, HEAD_DIM:]
    wy_bb = _dot3(wy, bb, _TN)
    d = HEAD_DIM
    rowd = lax.broadcasted_iota(jnp.int32, (d, d), 0)
    cold = lax.broadcasted_iota(jnp.int32, (d, d), 1)
    g_t = jnp.where(rowd == cold, p_c, 0.0) - wy_bb[:d]
    h_loc_t = _dot3(v, kb, _TN) - wy_bb[d:]
    o = _dot3(rq, s0, _NT) + o_loc
    s_new = _dot3(s0, g_t) + h_loc_t
    return o, s_new


def _rwkv_body(cols_ref, prev0_ref, s0_ref, mu_ref, w0_ref, w2_ref, a0_ref, a2_ref, g2_ref,
               kk_ref, ka_ref, rk_ref, lnw_ref, lnb_ref, o_ref, s_out_ref, s_scr, prev_scr,
               *, c, n_valid, chain):
    i = pl.program_id(0)
    if chain:
        @pl.when(i == 0)
        def _():
            s_scr[...] = s0_ref[0]
            prev_scr[...] = prev0_ref[0]
        prev_row = prev_scr[...]
        read_state = lambda h: s_scr[h]
    else:
        prev_row = prev0_ref[0]
        read_state = lambda h: s0_ref[0, h]

    cols = cols_ref[0]
    row = lax.broadcasted_iota(jnp.int32, cols.shape, 0)
    prev = jnp.where(row == 0, prev_row, pltpu.roll(cols, 1, 0))
    xs = cols + (prev - cols) * mu_ref[...]
    i1, i2, i3 = RW_WIDTH, 2 * RW_WIDTH, 3 * RW_WIDTH
    i4, i5 = i3 + LORA_W, i3 + LORA_W + LORA_A
    r, k, v = xs[:, :i1], xs[:, i1:i2], xs[:, i2:i3]
    wl, al, gl = xs[:, i3:i4], xs[:, i4:i5], xs[:, i5:]
    z = w0_ref[...] + _dot1(jnp.tanh(wl), w2_ref[...])
    lw = -math.exp(-0.5) * _sigmoid(z)
    a = _sigmoid(a0_ref[...] + _dot1(al, a2_ref[...]))
    g = _dot1(_sigmoid(gl), g2_ref[...])
    kk_raw = k * kk_ref[...]
    k_mod = k * (1.0 + (a - 1.0) * ka_ref[...])
    if n_valid < c:
        valid = lax.broadcasted_iota(jnp.int32, (c, RW_WIDTH), 0) < n_valid
        lw = jnp.where(valid, lw, 0.0)
        kk_raw = jnp.where(valid, kk_raw, 0.0)
        k_mod = jnp.where(valid, k_mod, 0.0)
        v = jnp.where(valid, v, 0.0)

    outs = []
    for h in range(RW_HEADS):
        hs = slice(h * HEAD_DIM, (h + 1) * HEAD_DIM)
        kk_h = kk_raw[:, hs]
        kk_h = kk_h * lax.rsqrt(jnp.maximum(jnp.sum(kk_h * kk_h, axis=-1, keepdims=True), 1e-24))
        a_h, r_h, k_h, v_h = a[:, hs], r[:, hs], k_mod[:, hs], v[:, hs]
        o_h, s_new = _rwkv_head(r_h, k_h, v_h, kk_h, kk_h * a_h, lw[:, hs], read_state(h), c)
        if chain:
            s_scr[h] = s_new
        s_out_ref[0, h] = s_new
        mu = jnp.mean(o_h, axis=-1, keepdims=True)
        var = jnp.mean(jnp.square(o_h - mu), axis=-1, keepdims=True)
        o_h = (o_h - mu) * lax.rsqrt(var + GN_EPS)
        o_h = o_h * lnw_ref[:, hs] + lnb_ref[:, hs]
        o_h = o_h + jnp.sum(r_h * k_h * rk_ref[:, hs], axis=-1, keepdims=True) * v_h
        outs.append(o_h)
    o_ref[0] = jnp.concatenate(outs, axis=1) * g
    if chain:
        prev_scr[...] = cols[c - 1:c, :]


def _rwkv(cols3, prev0, s0, params, *, n_valid, chain):
    mu_shift, w0, w2, a0, a2, g2, k_k, k_a, r_k, ln_x_w, ln_x_b = params
    nb, c = cols3.shape[0], cols3.shape[1]
    row = lambda a: a.reshape(1, -1).astype(F32)
    per_step = (lambda i: (0, 0, 0)) if chain else (lambda i: (i, 0, 0))
    per_step4 = (lambda i: (0, 0, 0, 0)) if chain else (lambda i: (i, 0, 0, 0))
    full = lambda shape: pl.BlockSpec(shape, lambda i: (0,) * len(shape))
    body = functools.partial(_rwkv_body, c=c, n_valid=n_valid, chain=chain)
    o, s_out = pl.pallas_call(
        body,
        grid=(nb,),
        in_specs=[
            pl.BlockSpec((1, c, RW_COLS), lambda i: (i, 0, 0)),
            pl.BlockSpec((1, 1, RW_COLS), per_step),
            pl.BlockSpec((1, RW_HEADS, HEAD_DIM, HEAD_DIM), per_step4),
            full((1, RW_COLS)), full((1, RW_WIDTH)), full((LORA_W, RW_WIDTH)),
            full((1, RW_WIDTH)), full((LORA_A, RW_WIDTH)), full((LORA_G, RW_WIDTH)),
            full((1, RW_WIDTH)), full((1, RW_WIDTH)), full((1, RW_WIDTH)),
            full((1, RW_WIDTH)), full((1, RW_WIDTH)),
        ],
        out_specs=[
            pl.BlockSpec((1, c, RW_WIDTH), lambda i: (i, 0, 0)),
            pl.BlockSpec((1, RW_HEADS, HEAD_DIM, HEAD_DIM), per_step4),
        ],
        out_shape=[
            jax.ShapeDtypeStruct((nb, c, RW_WIDTH), F32),
            jax.ShapeDtypeStruct((1 if chain else nb, RW_HEADS, HEAD_DIM, HEAD_DIM), F32),
        ],
        scratch_shapes=[pltpu.VMEM((RW_HEADS, HEAD_DIM, HEAD_DIM), F32), pltpu.VMEM((1, RW_COLS), F32)],
        compiler_params=pltpu.CompilerParams(
            dimension_semantics=("arbitrary",), vmem_limit_bytes=VMEM_LIMIT),
    )(cols3, prev0, s0, row(mu_shift), row(w0), w2.astype(BF16), row(a0), a2.astype(BF16),
      g2.astype(BF16), row(k_k), row(k_a), row(r_k), row(ln_x_w), row(ln_x_b))
    return o, s_out


def _softmax_heads(q, k_all, v_all, mask, scale):
    outs, lses = [], []
    for h in range(HEADS_PER_GROUP):
        hs = slice(h * HEAD_DIM, (h + 1) * HEAD_DIM)
        s = _dot1(q[:, hs], k_all[:, hs], _NT) * scale
        s = jnp.where(mask, s, NEG_INF)
        m = jnp.max(s, axis=-1, keepdims=True)
        p = jnp.exp(s - m)
        den = jnp.sum(p, axis=-1, keepdims=True)
        outs.append(_dot1(p, v_all[:, hs]) / den)
        lses.append(jnp.broadcast_to(m + jnp.log(den), (q.shape[0], HEAD_DIM)))
    return jnp.concatenate(outs, axis=1), jnp.concatenate(lses, axis=1)


def _band_attn_body(q_ref, kp_ref, kc_ref, vp_ref, vc_ref, o_ref, lse_ref, *, w):
    n = pl.program_id(1)
    k_all = jnp.concatenate([kp_ref[...], kc_ref[...]], axis=0)
    v_all = jnp.concatenate([vp_ref[...], vc_ref[...]], axis=0)
    qi = lax.broadcasted_iota(jnp.int32, (w, 2 * w), 0)
    kj = lax.broadcasted_iota(jnp.int32, (w, 2 * w), 1)
    band = jnp.logical_and(kj >= qi, kj <= qi + w)
    has_prev = jnp.logical_or(n > 0, kj >= w)
    o, lse = _softmax_heads(q_ref[...], k_all, v_all, jnp.logical_and(band, has_prev), HEAD_DIM ** -0.5)
    o_ref[...] = o
    lse_ref[...] = lse


def _band_attention(proj, g, win, dil):
    s_len = proj.shape[0]
    w = win // dil
    assert s_len % (dil * w) == 0 and w % LANES == 0
    l_len = s_len // dil
    cb = IN_COLS // ATT_OUT
    view = proj.reshape(l_len, dil * IN_COLS)
    qb, kb, vb = (c0 // ATT_OUT + g for c0 in (Q_COL0, K_COL0, V_COL0))
    cur = lambda b: pl.BlockSpec((w, ATT_OUT), lambda r, n: (n, r * cb + b))
    prv = lambda b: pl.BlockSpec((w, ATT_OUT), lambda r, n: (jnp.maximum(n - 1, 0), r * cb + b))
    out_spec = pl.BlockSpec((w, ATT_OUT), lambda r, n: (n, r))
    o, lse = pl.pallas_call(
        functools.partial(_band_attn_body, w=w),
        grid=(dil, l_len // w),
        in_specs=[cur(qb), prv(kb), cur(kb), prv(vb), cur(vb)],
        out_specs=[out_spec, out_spec],
        out_shape=[jax.ShapeDtypeStruct((l_len, dil * ATT_OUT), F32)] * 2,
        compiler_params=pltpu.CompilerParams(
            dimension_semantics=("arbitrary", "arbitrary"), vmem_limit_bytes=VMEM_LIMIT),
    )(view, view, view, view, view)
    return o.reshape(s_len, ATT_OUT), lse.reshape(s_len, ATT_OUT)


def _cache_attn_body(q_ref, kn_ref, vn_ref, cache_ref, o_ref, lse_ref, *, dil, t, nsteps):
    q = q_ref[0]
    kn, vn = kn_ref[0], vn_ref[0]
    kv_w = 2 * ATT_OUT
    ncol = nsteps + _T_PAD
    col = lax.broadcasted_iota(jnp.int32, (_T_PAD, ncol), 1)
    jrow = lax.broadcasted_iota(jnp.int32, (_T_PAD, ncol), 0)
    inew = col - nsteps
    new_ok = jnp.logical_and(jnp.logical_and(inew >= 0, inew <= jrow),
                             jnp.logical_and(inew < t, (jrow - inew) % dil == 0))
    cache_ok = jnp.logical_and(col < nsteps, col >= jrow) if dil == 1 else col < nsteps
    mask = jnp.logical_or(cache_ok, new_ok)
    o_rows, lse_rows = [], []
    for j in range(t):
        base = 0 if dil == 1 else j * kv_w
        k_all = jnp.concatenate([cache_ref[0, :, base:base + ATT_OUT], kn], axis=0)
        v_all = jnp.concatenate([cache_ref[0, :, base + ATT_OUT:base + kv_w], vn], axis=0)
        o, lse = _softmax_heads(q, k_all, v_all, mask, HEAD_DIM ** -0.5)
        o_rows.append(o[j:j + 1])
        lse_rows.append(lse[j:j + 1])
    pad = jnp.zeros((_T_PAD - t, ATT_OUT), F32)
    o_ref[0] = jnp.concatenate(o_rows + [pad], axis=0)
    lse_ref[0] = jnp.concatenate(lse_rows + [pad], axis=0)


def _cache_attention(qkv_pad, cache, g, win, dil):
    db, t = cache.shape[0], DEC_SEQ
    nsteps = win // dil
    assert cache.shape[1] == win and (dil == 1 or dil >= t) and t <= _T_PAD
    kv_w = 2 * ATT_OUT
    view = cache.reshape(db, nsteps, dil * kv_w)
    blk_w = kv_w if dil == 1 else t * kv_w
    nb = ATT_WIDTH // ATT_OUT
    tok = lambda b: pl.BlockSpec((1, _T_PAD, ATT_OUT), lambda i: (i, 0, b))
    out_spec = pl.BlockSpec((1, _T_PAD, ATT_OUT), lambda i: (i, 0, 0))
    o, lse = pl.pallas_call(
        functools.partial(_cache_attn_body, dil=dil, t=t, nsteps=nsteps),
        grid=(db,),
        in_specs=[tok(g), tok(nb + g), tok(2 * nb + g),
                  pl.BlockSpec((1, nsteps, blk_w), lambda i: (i, 0, 0))],
        out_specs=[out_spec, out_spec],
        out_shape=[jax.ShapeDtypeStruct((db, _T_PAD, ATT_OUT), F32)] * 2,
        compiler_params=pltpu.CompilerParams(
            dimension_semantics=("arbitrary",), vmem_limit_bytes=VMEM_LIMIT),
    )(qkv_pad, qkv_pad, qkv_pad, view)
    return o[:, :t].reshape(db * t, ATT_OUT), lse[:, :t].reshape(db * t, ATT_OUT)


def _finish_body(x_ref, ga_ref, gb_ref, oa_ref, o1_ref, o2_ref, o3_ref, l1_ref, l2_ref, l3_ref,
                 woa_ref, wob_ref, wout_ref, n2_ref, wpq_ref, x1_ref, h2_ref, q_ref):
    l1, l2, l3 = l1_ref[...], l2_ref[...], l3_ref[...]
    mx = jnp.maximum(jnp.maximum(l1, l2), l3)
    w1, w2, w3 = jnp.exp(l1 - mx), jnp.exp(l2 - mx), jnp.exp(l3 - mx)
    o_b = (w1 * o1_ref[...] + w2 * o2_ref[...] + w3 * o3_ref[...]) / (w1 + w2 + w3)
    merged = (_sigmoid(ga_ref[...]) * _dot1(oa_ref[...], woa_ref[...])
              + _sigmoid(gb_ref[...]) * _dot1(o_b, wob_ref[...]))
    x1 = x_ref[...] + _dot1(merged, wout_ref[...])
    x1_ref[...] = x1
    ms = jnp.mean(x1 * x1, axis=-1, keepdims=True)
    h2 = (x1 * lax.rsqrt(ms + NORM_EPS) * n2_ref[...]).astype(BF16)
    h2_ref[...] = h2
    q_ref[...] = jnp.dot(h2, wpq_ref[...], preferred_element_type=F32)


def _finish(x2d, proj, o_a, o_groups, lse_groups, w_oa, w_ob, w_out, norm2_w, w_pq):
    n = x2d.shape[0]
    tm = min(n, 512)
    assert n % tm == 0
    pq = PEER_HEADS * PEER_DKEY
    tok = lambda width, b=0: pl.BlockSpec((tm, width), lambda i: (i, b))
    full = lambda shape: pl.BlockSpec(shape, lambda i: (0,) * len(shape))
    gcb = GATE_COL0 // D_MODEL
    return pl.pallas_call(
        _finish_body,
        grid=(n // tm,),
        in_specs=[tok(D_MODEL), tok(D_MODEL, gcb), tok(D_MODEL, gcb + 1), tok(RW_WIDTH)]
        + [tok(ATT_OUT)] * 6
        + [full((RW_WIDTH, D_MODEL)), full((ATT_OUT, D_MODEL)), full((D_MODEL, D_MODEL)),
           full((1, D_MODEL)), full((D_MODEL, pq))],
        out_specs=[tok(D_MODEL), tok(D_MODEL), tok(pq)],
        out_shape=[jax.ShapeDtypeStruct((n, D_MODEL), F32), jax.ShapeDtypeStruct((n, D_MODEL), BF16),
                   jax.ShapeDtypeStruct((n, pq), F32)],
        compiler_params=pltpu.CompilerParams(
            dimension_semantics=("arbitrary",), vmem_limit_bytes=VMEM_LIMIT),
    )(x2d, proj, proj, o_a, *o_groups, *lse_groups, w_oa.astype(BF16), w_ob.astype(BF16),
      w_out.astype(BF16), norm2_w.reshape(1, D_MODEL), w_pq.astype(BF16))


def _top_rows(s, count):
    rid = lax.broadcasted_iota(jnp.int32, (count, s.shape[1]), 0)
    top = jnp.zeros((count, s.shape[1]), F32)
    for i in range(count):
        m = jnp.max(s, axis=0, keepdims=True)
        top = jnp.where(rid == i, m, top)
        s = jnp.where(s == m, -jnp.inf, s)
    return top


def _route_body(q_ref, k1_ref, k2_ref, s1_ref, e1_ref, s2_ref, e2_ref, tau_ref):
    half = PEER_DKEY // 2
    taus = []
    for h in range(PEER_HEADS):
        q1 = q_ref[:, h * PEER_DKEY:h * PEER_DKEY + half]
        q2 = q_ref[:, h * PEER_DKEY + half:(h + 1) * PEER_DKEY]
        s1 = _dot1(k1_ref[...], q1, _NT)
        s2 = _dot1(k2_ref[...], q2, _NT)
        v1 = _top_rows(s1, PEER_TOPK)
        v2 = _top_rows(s2, PEER_TOPK)
        cand = jnp.concatenate([v1[i:i + 1, :] + v2 for i in range(PEER_TOPK)], axis=0)
        best = _top_rows(cand, PEER_TOPK)
        z = jnp.sum(jnp.exp(best - best[0:1, :]), axis=0, keepdims=True)
        s1_ref[h] = s1
        e1_ref[h] = jnp.exp(s1 - v1[0:1, :]) / z
        s2_ref[h] = s2
        e2_ref[h] = jnp.exp(s2 - v2[0:1, :])
        taus.append(best[PEER_TOPK - 1:PEER_TOPK, :])
    tau_ref[...] = jnp.concatenate(taus, axis=0)


def _route(q, sub_keys_1, sub_keys_2):
    n = q.shape[0]
    tr = min(n, 256)
    assert n % tr == 0
    pq = PEER_HEADS * PEER_DKEY
    big = pl.BlockSpec((PEER_HEADS, PEER_NKEYS, tr), lambda i: (0, 0, i))
    keys = pl.BlockSpec((PEER_NKEYS, PEER_DKEY // 2), lambda i: (0, 0))
    big_shape = jax.ShapeDtypeStruct((PEER_HEADS, PEER_NKEYS, n), F32)
    return pl.pallas_call(
        _route_body,
        grid=(n // tr,),
        in_specs=[pl.BlockSpec((tr, pq), lambda i: (i, 0)), keys, keys],
        out_specs=[big, big, big, big, pl.BlockSpec((PEER_HEADS, tr), lambda i: (0, i))],
        out_shape=[big_shape] * 4 + [jax.ShapeDtypeStruct((PEER_HEADS, n), F32)],
        compiler_params=pltpu.CompilerParams(
            dimension_semantics=("arbitrary",), vmem_limit_bytes=VMEM_LIMIT),
    )(q, sub_keys_1.astype(BF16), sub_keys_2.astype(BF16))


_F_TE = 512


def _dense_body(h2_ref, u_ref, vt_ref, s1_ref, e1_ref, s2_ref, e2_ref, tau_ref, x1_ref, nf_ref,
                y_ref, acc_ref):
    e = pl.program_id(1)

    @pl.when(e == 0)
    def _():
        acc_ref[...] = jnp.zeros_like(acc_ref)

    p_t = lax.dot_general(u_ref[...], h2_ref[...], _NT, preferred_element_type=F32)
    act = 0.5 * p_t * (1.0 + lax.erf(p_t * (2.0 ** -0.5)))
    slabs = []
    for al in range(_F_TE // PEER_NKEYS):
        a = e * (_F_TE // PEER_NKEYS) + al
        g = None
        for h in range(PEER_HEADS):
            s1_row = s1_ref[h, pl.ds(a, 1), :]
            e1_row = e1_ref[h, pl.ds(a, 1), :]
            sel = (s1_row + s2_ref[h]) >= tau_ref[h:h + 1, :]
            term = jnp.where(sel, e1_row * e2_ref[h], 0.0)
            g = term if g is None else g + term
        slabs.append(g)
    w_t = (jnp.concatenate(slabs, axis=0) * act).astype(BF16)
    acc_ref[...] += jnp.dot(vt_ref[...], w_t, preferred_element_type=F32)

    @pl.when(e == pl.num_programs(1) - 1)
    def _():
        out = x1_ref[...] + acc_ref[...].T
        ms = jnp.mean(out * out, axis=-1, keepdims=True)
        y_ref[...] = out * lax.rsqrt(ms + NORM_EPS) * nf_ref[...]


def _peer_dense(h2, x1, route, u_bf16, vt_bf16, normf_w):
    n = h2.shape[0]
    tm = min(n, 512)
    n_exp = u_bf16.shape[0]
    assert n % tm == 0 and n_exp % _F_TE == 0 and n_exp == PEER_NKEYS * PEER_NKEYS
    s1, e1, s2, e2, tau = route
    big = pl.BlockSpec((PEER_HEADS, PEER_NKEYS, tm), lambda i, e: (0, 0, i))
    tok = pl.BlockSpec((tm, D_MODEL), lambda i, e: (i, 0))
    return pl.pallas_call(
        _dense_body,
        grid=(n // tm, n_exp // _F_TE),
        in_specs=[tok, pl.BlockSpec((_F_TE, D_MODEL), lambda i, e: (e, 0)),
                  pl.BlockSpec((D_MODEL, _F_TE), lambda i, e: (0, e)),
                  big, big, big, big, pl.BlockSpec((PEER_HEADS, tm), lambda i, e: (0, i)),
                  tok, pl.BlockSpec((1, D_MODEL), lambda i, e: (0, 0))],
        out_specs=tok,
        out_shape=jax.ShapeDtypeStruct((n, D_MODEL), F32),
        scratch_shapes=[pltpu.VMEM((D_MODEL, tm), F32)],
        compiler_params=pltpu.CompilerParams(
            dimension_semantics=("arbitrary", "arbitrary"), vmem_limit_bytes=VMEM_LIMIT),
    )(h2, u_bf16, vt_bf16, s1, e1, s2, e2, tau, x1, normf_w.reshape(1, D_MODEL))


def _tail(x2d, proj, o_a, o_groups, lse_groups, out_params, normf_w, u_bf16, vt_bf16):
    w_oa, w_ob, w_out, norm2_w, w_pq, sub_keys_1, sub_keys_2 = out_params
    x1, h2, q = _finish(x2d, proj, o_a, o_groups, lse_groups, w_oa, w_ob, w_out, norm2_w, w_pq)
    route = _route(q, sub_keys_1, sub_keys_2)
    return _peer_dense(h2, x1, route, u_bf16, vt_bf16, normf_w)


def kernel(x_prompt, x_sample, cache_kv_g1, cache_kv_g2, cache_kv_g3, state_wkv, state_shift, norm1_w, w_in, mu_shift, w0, w2, a0, a2, g2, k_k, k_a, r_k, ln_x_w, ln_x_b, w_oA, w_oB, w_out, norm2_w, w_pq, sub_keys_1, sub_keys_2, expert_u, expert_v, normf_w):
    rw_params = (mu_shift, w0, w2, a0, a2, g2, k_k, k_a, r_k, ln_x_w, ln_x_b)
    out_params = (w_oA, w_oB, w_out, norm2_w, w_pq, sub_keys_1, sub_keys_2)
    caches = (cache_kv_g1, cache_kv_g2, cache_kv_g3)
    bsz, s_len, _ = x_prompt.shape
    db, dt, _ = x_sample.shape
    assert bsz == 1 and dt == DEC_SEQ
    w_in_b = w_in.astype(BF16)
    u_b = expert_u.astype(BF16)
    vt_b = expert_v.astype(BF16).T
    hg = HEADS_PER_GROUP

    xp = x_prompt[0]
    proj_p = _in_proj(xp, jnp.arange(s_len, dtype=F32), norm1_w, w_in_b)
    c = _RW_CHUNK
    o_a_p, wkv_p = _rwkv(proj_p.reshape(s_len // c, c, IN_COLS), jnp.zeros((1, 1, RW_COLS), F32),
                         jnp.zeros((1, RW_HEADS, HEAD_DIM, HEAD_DIM), F32), rw_params, n_valid=c, chain=True)
    att_p = [_band_attention(proj_p, g, win, dil) for g, (win, dil) in enumerate(ATT_GROUPS)]
    y_p = _tail(xp, proj_p, o_a_p.reshape(s_len, RW_WIDTH), [a[0] for a in att_p], [a[1] for a in att_p],
                out_params, normf_w, u_b, vt_b)
    p_kv = []
    for g, (win, _) in enumerate(ATT_GROUPS):
        keep = min(win, s_len)
        k_rows = proj_p[s_len - keep:, K_COL0 + g * ATT_OUT:K_COL0 + (g + 1) * ATT_OUT]
        v_rows = proj_p[s_len - keep:, V_COL0 + g * ATT_OUT:V_COL0 + (g + 1) * ATT_OUT]
        p_kv.append(jnp.stack([k_rows.reshape(1, keep, hg, HEAD_DIM), v_rows.reshape(1, keep, hg, HEAD_DIM)], axis=2))
    p_shift = proj_p[s_len - 1:, :RW_COLS]

    xs = x_sample.reshape(db * dt, D_MODEL)
    pos_s = jnp.tile(jnp.arange(dt, dtype=F32) + float(PAST_LEN), db)
    proj_s = _in_proj(xs, pos_s, norm1_w, w_in_b)
    proj_s3 = proj_s.reshape(db, dt, IN_COLS)
    pad_t = ((0, 0), (0, _T_PAD - dt), (0, 0))
    o_a_s, wkv_s = _rwkv(jnp.pad(proj_s3[:, :, :RW_COLS], pad_t), state_shift[:, None, :], state_wkv,
                         rw_params, n_valid=dt, chain=False)
    qkv_pad = jnp.pad(proj_s3[:, :, Q_COL0:GATE_COL0], pad_t)
    att_s = [_cache_attention(qkv_pad, caches[g], g, win, dil) for g, (win, dil) in enumerate(ATT_GROUPS)]
    y_s = _tail(xs, proj_s, o_a_s[:, :dt].reshape(db * dt, RW_WIDTH), [a[0] for a in att_s],
                [a[1] for a in att_s], out_params, normf_w, u_b, vt_b)
    s_kv = []
    for g in range(len(ATT_GROUPS)):
        k_new = proj_s3[:, :, K_COL0 + g * ATT_OUT:K_COL0 + (g + 1) * ATT_OUT].reshape(db, dt, hg, HEAD_DIM)
        v_new = proj_s3[:, :, V_COL0 + g * ATT_OUT:V_COL0 + (g + 1) * ATT_OUT].reshape(db, dt, hg, HEAD_DIM)
        s_kv.append(jnp.stack([k_new, v_new], axis=2))
    s_shift = proj_s3[:, dt - 1, :RW_COLS]

    return (y_p.reshape(1, s_len, D_MODEL), y_s.reshape(db, dt, D_MODEL), p_kv[0], p_kv[1], p_kv[2],
            wkv_p, p_shift, s_kv[0], s_kv[1], s_kv[2], wkv_s, s_shift)
```

```python
import functools
import math

import jax
import jax.numpy as jnp
from jax import lax
from jax.experimental import pallas as pl
from jax.experimental.pallas import tpu as pltpu

F32 = jnp.float32
BF16 = jnp.bfloat16

D_MODEL = 1024
HEAD_DIM = 64
NORM_EPS = 1e-6
NEG_INF = -1e30
RW_HEADS = 8
RW_WIDTH = RW_HEADS * HEAD_DIM
LORA_W = 64
LORA_A = 64
LORA_G = 128
RW_COLS = 3 * RW_WIDTH + LORA_W + LORA_A + LORA_G
GN_EPS = 64e-5
ATT_GROUPS = ((128, 1), (512, 4), (2048, 16))
HEADS_PER_GROUP = 4
ATT_HEADS = HEADS_PER_GROUP * len(ATT_GROUPS)
ATT_WIDTH = ATT_HEADS * HEAD_DIM
ATT_COLS = 3 * ATT_WIDTH
ATT_OUT = HEADS_PER_GROUP * HEAD_DIM
ROT_DIM = HEAD_DIM // 4
ROPE_THETA = 500000.0
GATE_COLS = 2 * D_MODEL
IN_COLS = RW_COLS + ATT_COLS + GATE_COLS
PEER_HEADS = 8
PEER_NKEYS = 128
PEER_DKEY = 256
PEER_TOPK = 16

DEC_SEQ = 4
PAST_LEN = 16384

LANES = 128
_T_PAD = 8
_RW_CHUNK = 64
Q_COL0 = RW_COLS
K_COL0 = RW_COLS + ATT_WIDTH
V_COL0 = RW_COLS + 2 * ATT_WIDTH
GATE_COL0 = RW_COLS + ATT_COLS
VMEM_LIMIT = 56 * 1024 * 1024

_NT = (((1,), (1,)), ((), ()))
_TN = (((0,), (0,)), ((), ()))
_NN = (((1,), (0,)), ((), ()))


def _dot1(a, b, dims=_NN):
    return lax.dot_general(a.astype(BF16), b.astype(BF16), dims, preferred_element_type=F32)


def _split(x):
    hi = x.astype(BF16)
    lo = (x - hi.astype(F32)).astype(BF16)
    return hi, lo


def _dot3(a, b, dims=_NN):
    ah, al = _split(a)
    bh, bl = _split(b)
    dg = functools.partial(lax.dot_general, dimension_numbers=dims, preferred_element_type=F32)
    return dg(ah, bh) + (dg(ah, bl) + dg(al, bh))


def _sigmoid(x):
    return 1.0 / (1.0 + jnp.exp(-x))


_A_TN = 256


def _inproj_body(x_ref, nw_ref, w_ref, c_ref, sa_ref, sb_ref, o_ref, h_scr):
    j = pl.program_id(1)

    @pl.when(j == 0)
    def _():
        x = x_ref[...]
        ms = jnp.mean(x * x, axis=-1, keepdims=True)
        h_scr[...] = (x * lax.rsqrt(ms + NORM_EPS) * nw_ref[...]).astype(BF16)

    acc = jnp.dot(h_scr[...], w_ref[...], preferred_element_type=F32)
    is_rope = jnp.logical_and(j >= Q_COL0 // _A_TN, j < V_COL0 // _A_TN)

    @pl.when(is_rope)
    def _():
        c, sa, sb = c_ref[...], sa_ref[...], sb_ref[...]
        for t in range(_A_TN // LANES):
            xh = acc[:, t * LANES:(t + 1) * LANES]
            o_ref[:, t * LANES:(t + 1) * LANES] = (
                xh * c + pltpu.roll(xh, LANES - ROT_DIM // 2, 1) * sa + pltpu.roll(xh, ROT_DIM // 2, 1) * sb)

    @pl.when(jnp.logical_not(is_rope))
    def _():
        o_ref[...] = acc


def _rope_tables(pos):
    half = ROT_DIM // 2
    inv_freq = ROPE_THETA ** (-jnp.arange(half, dtype=F32) * (2.0 / ROT_DIM))
    ang = pos[:, None] * inv_freq[None, :]
    cos, sin = jnp.cos(ang), jnp.sin(ang)
    t = pos.shape[0]
    one = jnp.ones((t, HEAD_DIM - ROT_DIM), F32)
    zero = jnp.zeros((t, HEAD_DIM - ROT_DIM), F32)
    z8 = jnp.zeros((t, half), F32)
    c = jnp.concatenate([cos, cos, one], axis=1)
    sa = jnp.concatenate([-sin, z8, zero], axis=1)
    sb = jnp.concatenate([z8, sin, zero], axis=1)
    rep = LANES // HEAD_DIM
    return tuple(jnp.tile(a, (1, rep)) for a in (c, sa, sb))


def _in_proj(x2d, pos, norm1_w, w_in_bf16):
    n = x2d.shape[0]
    tm = min(n, 1024)
    assert n % tm == 0 and Q_COL0 % _A_TN == 0 and V_COL0 % _A_TN == 0
    c, sa, sb = _rope_tables(pos)
    tab_spec = pl.BlockSpec((tm, LANES), lambda i, j: (i, 0))
    return pl.pallas_call(
        _inproj_body,
        name="inproj",
        grid=(n // tm, IN_COLS // _A_TN),
        in_specs=[
            pl.BlockSpec((tm, D_MODEL), lambda i, j: (i, 0)),
            pl.BlockSpec((1, D_MODEL), lambda i, j: (0, 0)),
            pl.BlockSpec((D_MODEL, _A_TN), lambda i, j: (0, j)),
            tab_spec, tab_spec, tab_spec,
        ],
        out_specs=pl.BlockSpec((tm, _A_TN), lambda i, j: (i, j)),
        out_shape=jax.ShapeDtypeStruct((n, IN_COLS), F32),
        scratch_shapes=[pltpu.VMEM((tm, D_MODEL), BF16)],
        compiler_params=pltpu.CompilerParams(
            dimension_semantics=("arbitrary", "arbitrary"), vmem_limit_bytes=VMEM_LIMIT),
    )(x2d, norm1_w.reshape(1, D_MODEL), w_in_bf16, c, sa, sb)


def _cumsum_rows(x, c):
    row = lax.broadcasted_iota(jnp.int32, x.shape, 0)
    s = 1
    while s < c:
        x = x + jnp.where(row >= s, pltpu.roll(x, s, 0), 0.0)
        s *= 2
    return x


def _unit_lower_inverse(n_mats, c):
    row = lax.broadcasted_iota(jnp.int32, (c, c), 0)
    col = lax.broadcasted_iota(jnp.int32, (c, c), 1)
    eye = jnp.where(row == col, 1.0, 0.0)
    xs = [eye - n_mat for n_mat in n_mats]
    ps = [_dot3(n_mat, n_mat) for n_mat in n_mats]
    n = 2
    while n < c:
        xs = [x + _dot3(x, p) for x, p in zip(xs, ps)]
        n *= 2
        if n < c:
            ps = [_dot3(p, p) for p in ps]
    return xs


def _rwkv_heads(ins, h0s, c):
    d = HEAD_DIM
    row = lax.broadcasted_iota(jnp.int32, (c, c), 0)
    col = lax.broadcasted_iota(jnp.int32, (c, c), 1)
    strict = row > col
    incl = row >= col
    rowd = lax.broadcasted_iota(jnp.int32, (d, d), 0)
    cold = lax.broadcasted_iota(jnp.int32, (d, d), 1)
    scs = [_dot3(jnp.concatenate([kkt, rt], axis=0), jnp.concatenate([kh, bh], axis=0), _NT)
           for kkt, rt, kh, bh, *_ in ins]
    m_kv = [jnp.where(strict, sc[:c, :c], 0.0) for sc in scs]
    m_kb = [jnp.where(strict, sc[:c, c:], 0.0) for sc in scs]
    a_rk = [jnp.where(incl, sc[c:, :c], 0.0) for sc in scs]
    a_rb = [jnp.where(incl, sc[c:, c:], 0.0) for sc in scs]
    vs = [x[4] for x in ins]
    mv = [_dot3(m, v) for m, v in zip(m_kv, vs)]
    ark_v = [_dot3(m, v) for m, v in zip(a_rk, vs)]
    kb_v = [_dot3(x[5], x[4]) for x in ins]
    t_inv = _unit_lower_inverse(m_kb, c)
    wy = [_dot3(t, jnp.concatenate([x[0], m], axis=1)) for t, x, m in zip(t_inv, ins, mv)]
    arb_wy = [_dot3(m, w) for m, w in zip(a_rb, wy)]
    bwy = [_dot3(x[6], w) for x, w in zip(ins, wy)]
    outs = []
    for x, h0, aw, av, bw, kv in zip(ins, h0s, arb_wy, ark_v, bwy, kb_v):
        rq = x[1] - aw[:, :d]
        o_loc = av - aw[:, d:]
        g = jnp.where(rowd == cold, x[7], 0.0) - bw[:, :d]
        h_loc = kv - bw[:, d:]
        outs.append((_dot3(rq, h0) + o_loc, _dot3(g, h0) + h_loc))
    return outs


def _rwkv_body(cols_ref, prev0_ref, s0_ref, mu_ref, w0_ref, w2_ref, a0_ref, a2_ref, g2_ref,
               kk_ref, ka_ref, rk_ref, lnw_ref, lnb_ref, o_ref, s_out_ref, h_scr, prev_scr,
               *, c, n_valid, chain):
    i = pl.program_id(0)
    if chain:
        @pl.when(i == 0)
        def _():
            for h in range(RW_HEADS):
                h_scr[h] = s0_ref[0, h].T
            prev_scr[...] = prev0_ref[0]
        prev_row = prev_scr[...]
        read_state = lambda h: h_scr[h]
    else:
        prev_row = prev0_ref[0]
        read_state = lambda h: s0_ref[0, h].T

    cols = cols_ref[0]
    row = lax.broadcasted_iota(jnp.int32, cols.shape, 0)
    prev = jnp.where(row == 0, prev_row, pltpu.roll(cols, 1, 0))
    xs = cols + (prev - cols) * mu_ref[...]
    i1, i2, i3 = RW_WIDTH, 2 * RW_WIDTH, 3 * RW_WIDTH
    i4, i5 = i3 + LORA_W, i3 + LORA_W + LORA_A
    r, k, v = xs[:, :i1], xs[:, i1:i2], xs[:, i2:i3]
    wl, al, gl = xs[:, i3:i4], xs[:, i4:i5], xs[:, i5:]
    z = w0_ref[...] + _dot1(jnp.tanh(wl), w2_ref[...])
    lw = -math.exp(-0.5) * _sigmoid(z)
    a = _sigmoid(a0_ref[...] + _dot1(al, a2_ref[...]))
    g = _dot1(_sigmoid(gl), g2_ref[...])
    kk_raw = k * kk_ref[...]
    k_mod = k * (1.0 + (a - 1.0) * ka_ref[...])
    if n_valid < c:
        valid = lax.broadcasted_iota(jnp.int32, (c, RW_WIDTH), 0) < n_valid
        lw = jnp.where(valid, lw, 0.0)
        kk_raw = jnp.where(valid, kk_raw, 0.0)
        k_mod = jnp.where(valid, k_mod, 0.0)
        v = jnp.where(valid, v, 0.0)

    heads = [slice(h * HEAD_DIM, (h + 1) * HEAD_DIM) for h in range(RW_HEADS)]
    inv_norm = [lax.rsqrt(jnp.maximum(jnp.sum(jnp.square(kk_raw[:, hs]), axis=-1, keepdims=True), 1e-24))
                for hs in heads]
    kk = jnp.concatenate([kk_raw[:, hs] * n for hs, n in zip(heads, inv_norm)], axis=1)
    b = kk * a

    cum = _cumsum_rows(lw, c)
    cl = cum[c - 1:c, :]
    e_tail = jnp.exp(cl - cum)
    e_neg = jnp.exp(-cum)
    kkt = kk * jnp.exp(cum - lw)
    rt = r * jnp.exp(cum)
    kh = k_mod * e_neg
    bh = b * e_neg
    kb_t = (k_mod * e_tail).T
    bb_t = (b * e_tail).T
    p_c = jnp.exp(cl)

    ins = [(kkt[:, hs], rt[:, hs], kh[:, hs], bh[:, hs], v[:, hs], kb_t[hs, :], bb_t[hs, :], p_c[:, hs])
           for hs in heads]
    results = _rwkv_heads(ins, [read_state(h) for h in range(RW_HEADS)], c)
    outs = []
    for h, hs in enumerate(heads):
        o_h, h_new = results[h]
        if chain:
            h_scr[h] = h_new
        else:
            s_out_ref[0, h] = h_new.T
        mu = jnp.mean(o_h, axis=-1, keepdims=True)
        var = jnp.mean(jnp.square(o_h - mu), axis=-1, keepdims=True)
        o_h = (o_h - mu) * lax.rsqrt(var + GN_EPS)
        o_h = o_h * lnw_ref[:, hs] + lnb_ref[:, hs]
        o_h = o_h + jnp.sum(r[:, hs] * k_mod[:, hs] * rk_ref[:, hs], axis=-1, keepdims=True) * v[:, hs]
        outs.append(o_h)
    o_ref[0] = jnp.concatenate(outs, axis=1) * g
    if chain:
        prev_scr[...] = cols[c - 1:c, :]

        @pl.when(i == pl.num_programs(0) - 1)
        def _():
            for h in range(RW_HEADS):
                s_out_ref[0, h] = h_scr[h].T


def _rwkv(cols3, prev0, s0, params, *, n_valid, chain):
    mu_shift, w0, w2, a0, a2, g2, k_k, k_a, r_k, ln_x_w, ln_x_b = params
    nb, c = cols3.shape[0], cols3.shape[1]
    row = lambda a: a.reshape(1, -1).astype(F32)
    per_step = (lambda i: (0, 0, 0)) if chain else (lambda i: (i, 0, 0))
    per_step4 = (lambda i: (0, 0, 0, 0)) if chain else (lambda i: (i, 0, 0, 0))
    full = lambda shape: pl.BlockSpec(shape, lambda i: (0,) * len(shape))
    body = functools.partial(_rwkv_body, c=c, n_valid=n_valid, chain=chain)
    o, s_out = pl.pallas_call(
        body,
        name="rwkv_chain" if chain else "rwkv_batch",
        grid=(nb,),
        in_specs=[
            pl.BlockSpec((1, c, RW_COLS), lambda i: (i, 0, 0)),
            pl.BlockSpec((1, 1, RW_COLS), per_step),
            pl.BlockSpec((1, RW_HEADS, HEAD_DIM, HEAD_DIM), per_step4),
            full((1, RW_COLS)), full((1, RW_WIDTH)), full((LORA_W, RW_WIDTH)),
            full((1, RW_WIDTH)), full((LORA_A, RW_WIDTH)), full((LORA_G, RW_WIDTH)),
            full((1, RW_WIDTH)), full((1, RW_WIDTH)), full((1, RW_WIDTH)),
            full((1, RW_WIDTH)), full((1, RW_WIDTH)),
        ],
        out_specs=[
            pl.BlockSpec((1, c, RW_WIDTH), lambda i: (i, 0, 0)),
            pl.BlockSpec((1, RW_HEADS, HEAD_DIM, HEAD_DIM), per_step4),
        ],
        out_shape=[
            jax.ShapeDtypeStruct((nb, c, RW_WIDTH), F32),
            jax.ShapeDtypeStruct((1 if chain else nb, RW_HEADS, HEAD_DIM, HEAD_DIM), F32),
        ],
        scratch_shapes=[pltpu.VMEM((RW_HEADS, HEAD_DIM, HEAD_DIM), F32), pltpu.VMEM((1, RW_COLS), F32)],
        compiler_params=pltpu.CompilerParams(
            dimension_semantics=("arbitrary",), vmem_limit_bytes=VMEM_LIMIT),
    )(cols3, prev0, s0, row(mu_shift), row(w0), w2.astype(BF16), row(a0), a2.astype(BF16),
      g2.astype(BF16), row(k_k), row(k_a), row(r_k), row(ln_x_w), row(ln_x_b))
    return o, s_out


def _softmax_heads(q, k_all, v_all, mask, scale):
    outs, lses = [], []
    for h in range(q.shape[1] // HEAD_DIM):
        hs = slice(h * HEAD_DIM, (h + 1) * HEAD_DIM)
        s = _dot1(q[:, hs], k_all[:, hs], _NT) * scale
        s = jnp.where(mask, s, NEG_INF)
        m = jnp.max(s, axis=-1, keepdims=True)
        p = jnp.exp(s - m)
        den = jnp.sum(p, axis=-1, keepdims=True)
        outs.append(_dot1(p, v_all[:, hs]) / den)
        lses.append(jnp.broadcast_to(m + jnp.log(den), (q.shape[0], HEAD_DIM)))
    return jnp.concatenate(outs, axis=1), jnp.concatenate(lses, axis=1)


def _band_attn_body(q_ref, kp_ref, kc_ref, vp_ref, vc_ref, o_ref, lse_ref, *, w, dil):
    n = pl.program_id(0)
    qi = lax.broadcasted_iota(jnp.int32, (w, 2 * w), 0)
    kj = lax.broadcasted_iota(jnp.int32, (w, 2 * w), 1)
    band = jnp.logical_and(kj >= qi, kj <= qi + w)
    mask = jnp.logical_and(band, jnp.logical_or(n > 0, kj >= w))

    def residue(r):
        rows = pl.ds(r, w, stride=dil) if dil > 1 else slice(None)
        k_all = jnp.concatenate([kp_ref[rows, :], kc_ref[rows, :]], axis=0)
        v_all = jnp.concatenate([vp_ref[rows, :], vc_ref[rows, :]], axis=0)
        o, lse = _softmax_heads(q_ref[rows, :], k_all, v_all, mask, HEAD_DIM ** -0.5)
        o_ref[rows, :] = o
        lse_ref[rows, :] = lse

    if dil == 1:
        residue(0)
    else:
        @pl.loop(0, dil)
        def _(r):
            residue(r)


def _band_attention(proj, g, win, dil):
    s_len = proj.shape[0]
    w = win // dil
    blk = dil * w
    assert s_len % blk == 0 and w % LANES == 0
    hp = ATT_OUT // LANES
    qb, kb, vb = ((c0 + g * ATT_OUT) // LANES for c0 in (Q_COL0, K_COL0, V_COL0))
    cur = lambda b: pl.BlockSpec((blk, LANES), lambda n, p: (n, b + p))
    prv = lambda b: pl.BlockSpec((blk, LANES), lambda n, p: (jnp.maximum(n - 1, 0), b + p))
    out_spec = pl.BlockSpec((blk, LANES), lambda n, p: (n, p))
    return pl.pallas_call(
        functools.partial(_band_attn_body, w=w, dil=dil),
        name=f"band_attn_g{g}",
        grid=(s_len // blk, hp),
        in_specs=[cur(qb), prv(kb), cur(kb), prv(vb), cur(vb)],
        out_specs=[out_spec, out_spec],
        out_shape=[jax.ShapeDtypeStruct((s_len, ATT_OUT), F32)] * 2,
        compiler_params=pltpu.CompilerParams(
            dimension_semantics=("arbitrary", "arbitrary"), vmem_limit_bytes=VMEM_LIMIT),
    )(proj, proj, proj, proj, proj)


def _cache_attn_body(q_ref, kn_ref, vn_ref, cache_ref, o_ref, lse_ref, *, dil, t, nsteps):
    q = q_ref[0]
    kn, vn = kn_ref[0], vn_ref[0]
    kv_w = 2 * ATT_OUT
    ncol = nsteps + _T_PAD
    col = lax.broadcasted_iota(jnp.int32, (_T_PAD, ncol), 1)
    jrow = lax.broadcasted_iota(jnp.int32, (_T_PAD, ncol), 0)
    inew = col - nsteps
    new_ok = jnp.logical_and(jnp.logical_and(inew >= 0, inew <= jrow),
                             jnp.logical_and(inew < t, (jrow - inew) % dil == 0))
    cache_ok = jnp.logical_and(col < nsteps, col >= jrow) if dil == 1 else col < nsteps
    mask = jnp.logical_or(cache_ok, new_ok)
    o_rows, lse_rows = [], []
    for j in range(t):
        base = 0 if dil == 1 else j * kv_w
        k_all = jnp.concatenate([cache_ref[0, :, base:base + ATT_OUT], kn], axis=0)
        v_all = jnp.concatenate([cache_ref[0, :, base + ATT_OUT:base + kv_w], vn], axis=0)
        o, lse = _softmax_heads(q, k_all, v_all, mask, HEAD_DIM ** -0.5)
        o_rows.append(o[j:j + 1])
        lse_rows.append(lse[j:j + 1])
    pad = jnp.zeros((_T_PAD - t, ATT_OUT), F32)
    o_ref[0] = jnp.concatenate(o_rows + [pad], axis=0)
    lse_ref[0] = jnp.concatenate(lse_rows + [pad], axis=0)


def _cache_attention(qkv_pad, cache, g, win, dil):
    db, t = cache.shape[0], DEC_SEQ
    nsteps = win // dil
    assert cache.shape[1] == win and (dil == 1 or dil >= t) and t <= _T_PAD
    kv_w = 2 * ATT_OUT
    view = cache.reshape(db, nsteps, dil * kv_w)
    blk_w = kv_w if dil == 1 else t * kv_w
    nb = ATT_WIDTH // ATT_OUT
    tok = lambda b: pl.BlockSpec((1, _T_PAD, ATT_OUT), lambda i: (i, 0, b))
    out_spec = pl.BlockSpec((1, _T_PAD, ATT_OUT), lambda i: (i, 0, 0))
    o, lse = pl.pallas_call(
        functools.partial(_cache_attn_body, dil=dil, t=t, nsteps=nsteps),
        name=f"cache_attn_g{g}",
        grid=(db,),
        in_specs=[tok(g), tok(nb + g), tok(2 * nb + g),
                  pl.BlockSpec((1, nsteps, blk_w), lambda i: (i, 0, 0))],
        out_specs=[out_spec, out_spec],
        out_shape=[jax.ShapeDtypeStruct((db, _T_PAD, ATT_OUT), F32)] * 2,
        compiler_params=pltpu.CompilerParams(
            dimension_semantics=("arbitrary",), vmem_limit_bytes=VMEM_LIMIT),
    )(qkv_pad, qkv_pad, qkv_pad, view)
    return o[:, :t].reshape(db * t, ATT_OUT), lse[:, :t].reshape(db * t, ATT_OUT)


def _finish_body(x_ref, ga_ref, gb_ref, oa_ref, o1_ref, o2_ref, o3_ref, l1_ref, l2_ref, l3_ref,
                 woa_ref, wob_ref, wout_ref, n2_ref, wpq_ref, x1_ref, h2t_ref, q_ref):
    l1, l2, l3 = l1_ref[...], l2_ref[...], l3_ref[...]
    mx = jnp.maximum(jnp.maximum(l1, l2), l3)
    w1, w2, w3 = jnp.exp(l1 - mx), jnp.exp(l2 - mx), jnp.exp(l3 - mx)
    o_b = (w1 * o1_ref[...] + w2 * o2_ref[...] + w3 * o3_ref[...]) / (w1 + w2 + w3)
    merged = (_sigmoid(ga_ref[...]) * _dot1(oa_ref[...], woa_ref[...])
              + _sigmoid(gb_ref[...]) * _dot1(o_b, wob_ref[...]))
    x1 = x_ref[...] + _dot1(merged, wout_ref[...])
    x1_ref[...] = x1
    ms = jnp.mean(x1 * x1, axis=-1, keepdims=True)
    h2 = x1 * lax.rsqrt(ms + NORM_EPS) * n2_ref[...]
    h2t_ref[...] = h2.T.astype(BF16)
    q_ref[...] = _dot1(h2, wpq_ref[...])


def _finish(x2d, proj, o_a, o_groups, lse_groups, w_oa, w_ob, w_out, norm2_w, w_pq):
    n = x2d.shape[0]
    tm = min(n, 512)
    assert n % tm == 0
    pq = PEER_HEADS * PEER_DKEY
    tok = lambda width, b=0: pl.BlockSpec((tm, width), lambda i: (i, b))
    full = lambda shape: pl.BlockSpec(shape, lambda i: (0,) * len(shape))
    gcb = GATE_COL0 // D_MODEL
    return pl.pallas_call(
        _finish_body,
        name="finish",
        grid=(n // tm,),
        in_specs=[tok(D_MODEL), tok(D_MODEL, gcb), tok(D_MODEL, gcb + 1), tok(RW_WIDTH)]
        + [tok(ATT_OUT)] * 6
        + [full((RW_WIDTH, D_MODEL)), full((ATT_OUT, D_MODEL)), full((D_MODEL, D_MODEL)),
           full((1, D_MODEL)), full((D_MODEL, pq))],
        out_specs=[tok(D_MODEL), pl.BlockSpec((D_MODEL, tm), lambda i: (0, i)), tok(pq)],
        out_shape=[jax.ShapeDtypeStruct((n, D_MODEL), F32), jax.ShapeDtypeStruct((D_MODEL, n), BF16),
                   jax.ShapeDtypeStruct((n, pq), F32)],
        compiler_params=pltpu.CompilerParams(
            dimension_semantics=("arbitrary",), vmem_limit_bytes=VMEM_LIMIT),
    )(x2d, proj, proj, o_a, *o_groups, *lse_groups, w_oa.astype(BF16), w_ob.astype(BF16),
      w_out.astype(BF16), norm2_w.reshape(1, D_MODEL), w_pq.astype(BF16))


def _top_rows(s, count, with_rank=False):
    rid = lax.broadcasted_iota(jnp.int32, (count, s.shape[1]), 0)
    top = jnp.zeros((count, s.shape[1]), F32)
    rank = jnp.full(s.shape, float(count), F32)
    for i in range(count):
        m = jnp.max(s, axis=0, keepdims=True)
        top = jnp.where(rid == i, m, top)
        hit = s == m
        if with_rank:
            rank = jnp.where(hit, float(i), rank)
        s = jnp.where(hit, -jnp.inf, s)
    return (top, rank) if with_rank else top


def _route_body(q_ref, k1_ref, k2_ref, cnt_ref, e1_ref, rank_ref, e2_ref):
    half = PEER_DKEY // 2
    k = PEER_TOPK
    for h in range(PEER_HEADS):
        q1 = q_ref[:, h * PEER_DKEY:h * PEER_DKEY + half]
        q2 = q_ref[:, h * PEER_DKEY + half:(h + 1) * PEER_DKEY]
        s1 = _dot1(k1_ref[...], q1, _NT)
        s2 = _dot1(k2_ref[...], q2, _NT)
        v1 = _top_rows(s1, k)
        v2, rank2 = _top_rows(s2, k, with_rank=True)
        cand = jnp.concatenate([v1[0:1, :] + v2] + [v1[i:i + 1, :] + v2[:k // 2, :] for i in range(1, k // 2)]
                               + [v1[k // 2:, :] + v2[0:1, :]], axis=0)
        best = _top_rows(cand, k)
        tau = best[k - 1:k, :]
        z = jnp.sum(jnp.exp(best - best[0:1, :]), axis=0, keepdims=True)
        cnt = jnp.zeros(s1.shape, F32)
        for j in range(k):
            cnt = cnt + jnp.where(s1 + v2[j:j + 1, :] >= tau, 1.0, 0.0)
        cnt_ref[h] = cnt
        e1_ref[h] = jnp.exp(s1 - v1[0:1, :]) / z
        rank_ref[h] = rank2.astype(BF16)
        e2_ref[h] = jnp.exp(s2 - v2[0:1, :]).astype(BF16)


def _route(q, sub_keys_1, sub_keys_2):
    n = q.shape[0]
    tr = min(n, 256)
    assert n % tr == 0
    pq = PEER_HEADS * PEER_DKEY
    big = pl.BlockSpec((PEER_HEADS, PEER_NKEYS, tr), lambda i: (0, 0, i))
    keys = pl.BlockSpec((PEER_NKEYS, PEER_DKEY // 2), lambda i: (0, 0))
    shape = lambda dt: jax.ShapeDtypeStruct((PEER_HEADS, PEER_NKEYS, n), dt)
    return pl.pallas_call(
        _route_body,
        name="peer_route",
        grid=(n // tr,),
        in_specs=[pl.BlockSpec((tr, pq), lambda i: (i, 0)), keys, keys],
        out_specs=[big, big, big, big],
        out_shape=[shape(F32), shape(F32), shape(BF16), shape(BF16)],
        compiler_params=pltpu.CompilerParams(
            dimension_semantics=("arbitrary",), vmem_limit_bytes=VMEM_LIMIT),
    )(q, sub_keys_1.astype(BF16), sub_keys_2.astype(BF16))


_F_TE = 1024


def _dense_body(h2t_ref, u_ref, vt_ref, cnt_ref, e1_ref, rank_ref, e2_ref, x1_ref, nf_ref,
                y_ref, acc_ref):
    e = pl.program_id(1)

    @pl.when(e == 0)
    def _():
        acc_ref[...] = jnp.zeros_like(acc_ref)

    tm = h2t_ref.shape[1]
    pk = 16
    zero = jnp.zeros((), BF16)
    pb = jnp.dot(u_ref[...], h2t_ref[...], preferred_element_type=F32).astype(BF16)
    act = (pb * 0.5) * (1.0 + lax.erf(pb * (2.0 ** -0.5)))
    slabs = []
    for al in range(_F_TE // PEER_NKEYS):
        g = None
        for h in range(PEER_HEADS):
            cnt_row = jnp.broadcast_to(cnt_ref[h, al:al + 1, :], (pk, tm)).astype(BF16)[None]
            e1_row = jnp.broadcast_to(e1_ref[h, al:al + 1, :], (pk, tm)).astype(BF16)[None]
            rank3 = rank_ref[h].reshape(PEER_NKEYS // pk, pk, tm)
            e2_3 = e2_ref[h].reshape(PEER_NKEYS // pk, pk, tm)
            term = jnp.where(rank3 < cnt_row, e2_3 * e1_row, zero)
            g = term if g is None else g + term
        slabs.append(g.reshape(PEER_NKEYS, tm))
    w_t = jnp.concatenate(slabs, axis=0) * act
    acc_ref[...] += jnp.dot(vt_ref[...], w_t, preferred_element_type=F32)

    @pl.when(e == pl.num_programs(1) - 1)
    def _():
        out = x1_ref[...] + acc_ref[...].T
        ms = jnp.mean(out * out, axis=-1, keepdims=True)
        y_ref[...] = out * lax.rsqrt(ms + NORM_EPS) * nf_ref[...]


def _peer_dense(h2t, x1, route, u_bf16, vt_bf16, normf_w):
    n = x1.shape[0]
    tm = min(n, 512)
    n_exp = u_bf16.shape[0]
    ta = _F_TE // PEER_NKEYS
    assert n % tm == 0 and n_exp % _F_TE == 0 and n_exp == PEER_NKEYS * PEER_NKEYS and ta % 8 == 0
    cnt, e1, rank2, e2 = route
    rows_a = pl.BlockSpec((PEER_HEADS, ta, tm), lambda i, e: (0, e, i))
    all_b = pl.BlockSpec((PEER_HEADS, PEER_NKEYS, tm), lambda i, e: (0, 0, i))
    tok = pl.BlockSpec((tm, D_MODEL), lambda i, e: (i, 0))
    return pl.pallas_call(
        _dense_body,
        name="peer_dense",
        grid=(n // tm, n_exp // _F_TE),
        in_specs=[pl.BlockSpec((D_MODEL, tm), lambda i, e: (0, i)),
                  pl.BlockSpec((_F_TE, D_MODEL), lambda i, e: (e, 0)),
                  pl.BlockSpec((D_MODEL, _F_TE), lambda i, e: (0, e)),
                  rows_a, rows_a, all_b, all_b,
                  tok, pl.BlockSpec((1, D_MODEL), lambda i, e: (0, 0))],
        out_specs=tok,
        out_shape=jax.ShapeDtypeStruct((n, D_MODEL), F32),
        scratch_shapes=[pltpu.VMEM((D_MODEL, tm), F32)],
        compiler_params=pltpu.CompilerParams(
            dimension_semantics=("arbitrary", "arbitrary"), vmem_limit_bytes=VMEM_LIMIT),
    )(h2t, u_bf16, vt_bf16, cnt, e1, rank2, e2, x1, normf_w.reshape(1, D_MODEL))


def _tail(x2d, proj, o_a, o_groups, lse_groups, out_params, normf_w, u_bf16, vt_bf16):
    w_oa, w_ob, w_out, norm2_w, w_pq, sub_keys_1, sub_keys_2 = out_params
    x1, h2t, q = _finish(x2d, proj, o_a, o_groups, lse_groups, w_oa, w_ob, w_out, norm2_w, w_pq)
    route = _route(q, sub_keys_1, sub_keys_2)
    return _peer_dense(h2t, x1, route, u_bf16, vt_bf16, normf_w)


def kernel(x_prompt, x_sample, cache_kv_g1, cache_kv_g2, cache_kv_g3, state_wkv, state_shift, norm1_w, w_in, mu_shift, w0, w2, a0, a2, g2, k_k, k_a, r_k, ln_x_w, ln_x_b, w_oA, w_oB, w_out, norm2_w, w_pq, sub_keys_1, sub_keys_2, expert_u, expert_v, normf_w):
    rw_params = (mu_shift, w0, w2, a0, a2, g2, k_k, k_a, r_k, ln_x_w, ln_x_b)
    out_params = (w_oA, w_oB, w_out, norm2_w, w_pq, sub_keys_1, sub_keys_2)
    caches = (cache_kv_g1, cache_kv_g2, cache_kv_g3)
    bsz, s_len, _ = x_prompt.shape
    db, dt, _ = x_sample.shape
    assert bsz == 1 and dt == DEC_SEQ
    w_in_b = w_in.astype(BF16)
    u_b = expert_u.astype(BF16)
    vt_b = expert_v.astype(BF16).T
    hg = HEADS_PER_GROUP

    xp = x_prompt[0]
    proj_p = _in_proj(xp, jnp.arange(s_len, dtype=F32), norm1_w, w_in_b)
    c = _RW_CHUNK
    o_a_p, wkv_p = _rwkv(proj_p.reshape(s_len // c, c, IN_COLS), jnp.zeros((1, 1, RW_COLS), F32),
                         jnp.zeros((1, RW_HEADS, HEAD_DIM, HEAD_DIM), F32), rw_params, n_valid=c, chain=True)
    att_p = [_band_attention(proj_p, g, win, dil) for g, (win, dil) in enumerate(ATT_GROUPS)]
    y_p = _tail(xp, proj_p, o_a_p.reshape(s_len, RW_WIDTH), [a[0] for a in att_p], [a[1] for a in att_p],
                out_params, normf_w, u_b, vt_b)
    p_kv = []
    for g, (win, _) in enumerate(ATT_GROUPS):
        keep = min(win, s_len)
        k_rows = proj_p[s_len - keep:, K_COL0 + g * ATT_OUT:K_COL0 + (g + 1) * ATT_OUT]
        v_rows = proj_p[s_len - keep:, V_COL0 + g * ATT_OUT:V_COL0 + (g + 1) * ATT_OUT]
        p_kv.append(jnp.stack([k_rows.reshape(1, keep, hg, HEAD_DIM), v_rows.reshape(1, keep, hg, HEAD_DIM)], axis=2))
    p_shift = proj_p[s_len - 1:, :RW_COLS]

    xs = x_sample.reshape(db * dt, D_MODEL)
    pos_s = jnp.tile(jnp.arange(dt, dtype=F32) + float(PAST_LEN), db)
    proj_s = _in_proj(xs, pos_s, norm1_w, w_in_b)
    proj_s3 = proj_s.reshape(db, dt, IN_COLS)
    pad_t = ((0, 0), (0, _T_PAD - dt), (0, 0))
    o_a_s, wkv_s = _rwkv(jnp.pad(proj_s3[:, :, :RW_COLS], pad_t), state_shift[:, None, :], state_wkv,
                         rw_params, n_valid=dt, chain=False)
    qkv_pad = jnp.pad(proj_s3[:, :, Q_COL0:GATE_COL0], pad_t)
    att_s = [_cache_attention(qkv_pad, caches[g], g, win, dil) for g, (win, dil) in enumerate(ATT_GROUPS)]
    y_s = _tail(xs, proj_s, o_a_s[:, :dt].reshape(db * dt, RW_WIDTH), [a[0] for a in att_s],
                [a[1] for a in att_s], out_params, normf_w, u_b, vt_b)
    s_kv = []
    for g in range(len(ATT_GROUPS)):
        k_new = proj_s3[:, :, K_COL0 + g * ATT_OUT:K_COL0 + (g + 1) * ATT_OUT].reshape(db, dt, hg, HEAD_DIM)
        v_new = proj_s3[:, :, V_COL0 + g * ATT_OUT:V_COL0 + (g + 1) * ATT_OUT].reshape(db, dt, hg, HEAD_DIM)
        s_kv.append(jnp.stack([k_new, v_new], axis=2))
    s_shift = proj_s3[:, dt - 1, :RW_COLS]

    return (y_p.reshape(1, s_len, D_MODEL), y_s.reshape(db, dt, D_MODEL), p_kv[0], p_kv[1], p_kv[2],
            wkv_p, p_shift, s_kv[0], s_kv[1], s_kv[2], wkv_s, s_shift)
```
